```python
import jax, jax.numpy as jnp
from jax import lax
import numpy as np

D_MODEL = 1024
BATCH = 8
SEQ = 4096
DEPTH = 4

GRID_W = 64
CTX_LEN = 256
MIX_WIDTH = D_MODEL
NA_HEADS = 8
NA_HEAD_DIM = 64
NA_WIDTH = NA_HEADS * NA_HEAD_DIM
NA_WIN_ROWS = 8
NA_WIN_COLS = 16
DN_HEADS = 4
DN_HEAD_DIM = 128
DN_WIDTH = DN_HEADS * DN_HEAD_DIM
DN_CONV = 5
DN_CHUNK = 64
FFN_DIM = 2816
N_MOD = 9
IN_COLS = 3 * NA_WIDTH + 4 * DN_WIDTH + 4 * DN_HEADS
ROPE_BASE = 10000.0
RMS_EPS = 1e-6

kernel_name = 'hybrid_na_gdn_macaron_dit'


def rms_norm(x, gain):
    xf = x.astype(jnp.float32)
    y = xf * lax.rsqrt(jnp.mean(xf * xf, axis=-1, keepdims=True) + RMS_EPS)
    return (y * gain.astype(jnp.float32)).astype(x.dtype)


def l2_normalise(u):
    return u * lax.rsqrt(jnp.sum(u * u, axis=-1, keepdims=True) + RMS_EPS)


def modulate(h, shift, scale):
    return h * (1 + scale) + shift


def swiglu(h, w_in, w_out):
    gate, up = jnp.split(h @ w_in, 2, axis=-1)
    return (jax.nn.silu(gate) * up) @ w_out


def macaron_half(x, mods, gain, w_in, w_out):
    shift, scale, gate = mods
    h = modulate(rms_norm(x, gain), shift, scale)
    return x + 0.5 * gate * swiglu(h, w_in, w_out)


def axial_rope_tables(n_tok, head_dim):
    n_freq = head_dim // 4
    inv_freq = ROPE_BASE ** (-jnp.arange(n_freq, dtype=jnp.float32) / n_freq)
    t = jnp.arange(n_tok)
    row = (t // GRID_W).astype(jnp.float32)[:, None] * inv_freq
    col = (t % GRID_W).astype(jnp.float32)[:, None] * inv_freq
    ang = jnp.concatenate([row, row, col, col], axis=-1)
    return jnp.cos(ang), jnp.sin(ang)


def apply_axial_rope(u, cos, sin):
    hd = u.shape[-1]
    half, quarter = hd // 2, hd // 4
    def rot_half(z):
        return jnp.concatenate([-z[..., quarter:], z[..., :quarter]], axis=-1)
    rotated = jnp.concatenate([rot_half(u[..., :half]), rot_half(u[..., half:])], axis=-1)
    return u * cos[:, None, :] + rotated * sin[:, None, :]


def centred_depthwise_conv(u, w):
    k_taps, ch = w.shape
    return lax.conv_general_dilated(
        u, w[:, None, :].astype(u.dtype), window_strides=(1,),
        padding=[(k_taps // 2, k_taps // 2)],
        dimension_numbers=('NWC', 'WIO', 'NWC'), feature_group_count=ch)


def na_prepare(p, q_gain, k_gain):
    b, t, _ = p.shape
    q, k, v = jnp.split(p[..., :3 * NA_WIDTH], 3, axis=-1)
    q = rms_norm(q.reshape(b, t, NA_HEADS, NA_HEAD_DIM), q_gain) * NA_HEAD_DIM ** -0.5
    k = rms_norm(k.reshape(b, t, NA_HEADS, NA_HEAD_DIM), k_gain)
    v = v.reshape(b, t, NA_HEADS, NA_HEAD_DIM)
    return q, k, v


def neighbourhood_attention(q, k, v, k_ctx, v_ctx, rpb):
    b, t, h, d = q.shape
    rows = t // GRID_W
    wr = min(NA_WIN_ROWS, rows)
    cols = jnp.arange(GRID_W)
    col_start = jnp.clip(cols - NA_WIN_COLS // 2, 0, GRID_W - NA_WIN_COLS)
    col_ok = (cols[None, :] >= col_start[:, None]) & (cols[None, :] < col_start[:, None] + NA_WIN_COLS)
    dc_idx = jnp.clip(cols[None, :] - cols[:, None] + NA_WIN_COLS - 1, 0, 2 * NA_WIN_COLS - 2)
    rpb_cols = rpb[:, :, dc_idx]
    k_grid = k.reshape(b, rows, GRID_W, h, d)
    v_grid = v.reshape(b, rows, GRID_W, h, d)
    q_rows = jnp.moveaxis(q.reshape(b, rows, GRID_W, h, d), 1, 0)
    n_lat = wr * GRID_W

    def row_block(args):
        q_r, r = args
        start = jnp.clip(r - wr // 2, 0, rows - wr)
        k_blk = lax.dynamic_slice_in_dim(k_grid, start, wr, axis=1)
        v_blk = lax.dynamic_slice_in_dim(v_grid, start, wr, axis=1)
        dr_idx = start + jnp.arange(wr) - r + NA_WIN_ROWS - 1
        bias = jnp.transpose(rpb_cols[:, dr_idx], (0, 2, 1, 3))
        s_lat = jnp.einsum('bqhd,brkhd->bhqrk', q_r, k_blk, preferred_element_type=jnp.float32)
        s_lat = jnp.where(col_ok[:, None, :], s_lat + bias.astype(jnp.float32), -jnp.inf)
        s_lat = s_lat.reshape(b, h, GRID_W, n_lat)
        s_ctx = jnp.einsum('bqhd,bkhd->bhqk', q_r, k_ctx, preferred_element_type=jnp.float32)
        p = jax.nn.softmax(jnp.concatenate([s_lat, s_ctx], axis=-1), axis=-1).astype(v.dtype)
        o = jnp.einsum('bhqk,bkhd->bqhd', p[..., :n_lat], v_blk.reshape(b, n_lat, h, d))
        return o + jnp.einsum('bhqk,bkhd->bqhd', p[..., n_lat:], v_ctx)

    o = lax.map(row_block, (q_rows, jnp.arange(rows)))
    return jnp.moveaxis(o, 0, 1).reshape(b, t, h * d)


def context_attention(q, k, v):
    b, t, h, d = q.shape
    s = jnp.einsum('bqhd,bkhd->bhqk', q, k, preferred_element_type=jnp.float32)
    p = jax.nn.softmax(s, axis=-1).astype(v.dtype)
    return jnp.einsum('bhqk,bkhd->bqhd', p, v).reshape(b, t, h * d)


def dn_prepare(p, conv_w):
    b, t, _ = p.shape
    off = 3 * NA_WIDTH
    qkv = jax.nn.silu(centred_depthwise_conv(p[..., off:off + 3 * DN_WIDTH], conv_w)).astype(jnp.float32)
    q, k, v = jnp.split(qkv, 3, axis=-1)
    q = l2_normalise(q.reshape(b, t, DN_HEADS, DN_HEAD_DIM)) * DN_HEAD_DIM ** -0.5
    k = l2_normalise(k.reshape(b, t, DN_HEADS, DN_HEAD_DIM))
    v = v.reshape(b, t, DN_HEADS, DN_HEAD_DIM)
    off = off + 3 * DN_WIDTH
    gate = p[..., off:off + DN_WIDTH]
    off = off + DN_WIDTH
    beta = jax.nn.sigmoid(p[..., off:off + 2 * DN_HEADS].astype(jnp.float32)).reshape(b, t, 2, DN_HEADS)
    a_raw = p[..., off + 2 * DN_HEADS:off + 4 * DN_HEADS].astype(jnp.float32).reshape(b, t, 2, DN_HEADS)
    return q, k, v, gate, beta, a_raw


def dn_log_decay(a_raw, a_log, dt_bias):
    return -jnp.exp(a_log.astype(jnp.float32)) * jax.nn.softplus(a_raw + dt_bias.astype(jnp.float32))


def gated_delta_rule_chunked(q, k, v, g, beta, state0):
    b, t, h, dk = q.shape
    dv = v.shape[-1]
    n_chunks = t // DN_CHUNK

    def chunks(u):
        u = u.reshape((b, n_chunks, DN_CHUNK, h) + u.shape[3:])
        return jnp.moveaxis(u, (1, 3), (0, 2))

    qc, kc, vc, bc = chunks(q), chunks(k), chunks(v), chunks(beta)
    gc = jnp.cumsum(chunks(g), axis=-1)
    idx = jnp.arange(DN_CHUNK)
    incl = idx[:, None] >= idx[None, :]
    strict = idx[:, None] > idx[None, :]
    gamma = jnp.exp(jnp.where(incl, gc[..., :, None] - gc[..., None, :], -jnp.inf))
    k_beta = kc * bc[..., None]
    a_mat = jnp.where(strict, jnp.einsum('nbhid,nbhjd->nbhij', k_beta, kc) * gamma, 0.0)
    rhs = jnp.concatenate([vc * bc[..., None], k_beta * jnp.exp(gc)[..., None]], axis=-1)
    sol = lax.linalg.triangular_solve(a_mat, rhs, left_side=True, lower=True, unit_diagonal=True)
    u, w = sol[..., :dv], sol[..., dv:]
    qk = jnp.einsum('nbhid,nbhjd->nbhij', qc, kc) * gamma

    def step(state, xs):
        q_i, k_i, u_i, w_i, qk_i, g_i = xs
        v_new = u_i - jnp.einsum('bhck,bhkv->bhcv', w_i, state)
        o_i = (jnp.einsum('bhck,bhkv->bhcv', q_i * jnp.exp(g_i)[..., None], state)
               + jnp.einsum('bhij,bhjv->bhiv', qk_i, v_new))
        g_last = g_i[..., -1:]
        state = (state * jnp.exp(g_last)[..., None]
                 + jnp.einsum('bhck,bhcv->bhkv', k_i * jnp.exp(g_last - g_i)[..., None], v_new))
        return state, o_i

    state, o = lax.scan(step, state0, (qc, kc, u, w, qk, gc))
    return jnp.moveaxis(o, (0, 2), (1, 3)).reshape(b, t, h, dv), state


def _rev(u, reverse):
    return u[:, ::-1] if reverse else u


def dn_output(o, gate, out_gain):
    b, t = o.shape[:2]
    g = jax.nn.silu(gate.reshape(b, t, DN_HEADS, DN_HEAD_DIM).astype(jnp.float32))
    return (rms_norm(o, out_gain) * g).reshape(b, t, DN_WIDTH).astype(gate.dtype)


def delta_heads(p_lat, p_ctx, conv_w, a_log, dt_bias, out_gain, rope_cos, rope_sin, ctx_out):
    ql, kl, vl, gate_l, beta_l, a_l = dn_prepare(p_lat, conv_w)
    qc, kc, vc, gate_c, beta_c, a_c = dn_prepare(p_ctx, conv_w)
    ql = apply_axial_rope(ql, rope_cos, rope_sin)
    kl = apply_axial_rope(kl, rope_cos, rope_sin)
    b = p_lat.shape[0]
    o_lat = jnp.zeros(vl.shape, jnp.float32)
    o_ctx = jnp.zeros(vc.shape, jnp.float32)
    for direction in range(2):
        rev = direction == 1
        g_l = dn_log_decay(a_l[..., direction, :], a_log[direction], dt_bias[direction])
        g_c = dn_log_decay(a_c[..., direction, :], a_log[direction], dt_bias[direction])
        s0 = jnp.zeros((b, DN_HEADS, DN_HEAD_DIM, DN_HEAD_DIM), jnp.float32)
        oc, s_ctx = gated_delta_rule_chunked(_rev(qc, rev), _rev(kc, rev), _rev(vc, rev),
                                             _rev(g_c, rev), _rev(beta_c[..., direction, :], rev), s0)
        ol, _ = gated_delta_rule_chunked(_rev(ql, rev), _rev(kl, rev), _rev(vl, rev),
                                         _rev(g_l, rev), _rev(beta_l[..., direction, :], rev), s_ctx)
        o_lat = o_lat + _rev(ol, rev)
        if ctx_out:
            o_ctx = o_ctx + _rev(oc, rev)
    y_lat = dn_output(o_lat, gate_l, out_gain)
    y_ctx = dn_output(o_ctx, gate_c, out_gain) if ctx_out else None
    return y_lat, y_ctx


def hybrid_mixer(h_lat, h_ctx, w_in, q_gain, k_gain, rpb, conv_w, a_log, dt_bias, out_gain,
                 rope_cos, rope_sin, ctx_out):
    p_lat = h_lat @ w_in
    p_ctx = h_ctx @ w_in
    qa_l, ka_l, va_l = na_prepare(p_lat, q_gain, k_gain)
    qa_c, ka_c, va_c = na_prepare(p_ctx, q_gain, k_gain)
    na_lat = neighbourhood_attention(qa_l, ka_l, va_l, ka_c, va_c, rpb)
    dn_lat, dn_ctx = delta_heads(p_lat, p_ctx, conv_w, a_log, dt_bias, out_gain,
                                 rope_cos, rope_sin, ctx_out)
    y_lat = jnp.concatenate([na_lat, dn_lat], axis=-1)
    if not ctx_out:
        return y_lat, None
    y_ctx = jnp.concatenate([context_attention(qa_c, ka_c, va_c), dn_ctx], axis=-1)
    return y_lat, y_ctx


def setup_inputs(seed: int = 0) -> dict:
    key = jax.random.key(seed)
    ks = jax.random.split(key, 24)
    L, D, F = DEPTH, D_MODEL, FFN_DIM

    def nrm(k, shape, scale):
        return jax.random.normal(k, shape, jnp.float32) * scale

    dt = jnp.exp(jax.random.uniform(ks[16], (L, 2, DN_HEADS), jnp.float32, np.log(1e-3), np.log(1e-1)))
    return {
        'x': nrm(ks[0], (BATCH, SEQ, D), 1.0),
        'c': nrm(ks[1], (BATCH, D), 1.0),
        'ctx': nrm(ks[2], (BATCH, CTX_LEN, D), 1.0),
        'c_ctx': nrm(ks[3], (D,), 1.0),
        'w_mod': nrm(ks[4], (L, D, N_MOD * D), 0.5 * D ** -0.5),
        'b_mod': nrm(ks[5], (L, N_MOD * D), 0.02),
        'norm_ffn1': 1.0 + nrm(ks[6], (L, D), 0.05),
        'w_ffn1_in': nrm(ks[7], (L, D, 2 * F), D ** -0.5),
        'w_ffn1_out': nrm(ks[8], (L, F, D), F ** -0.5),
        'norm_mix': 1.0 + nrm(ks[9], (L, D), 0.05),
        'w_in': nrm(ks[10], (L, D, IN_COLS), D ** -0.5),
        'na_q_gain': 1.0 + nrm(ks[11], (L, NA_HEAD_DIM), 0.05),
        'na_k_gain': 1.0 + nrm(ks[12], (L, NA_HEAD_DIM), 0.05),
        'na_rpb': nrm(ks[13], (L, NA_HEADS, 2 * NA_WIN_ROWS - 1, 2 * NA_WIN_COLS - 1), 0.1),
        'dn_conv': nrm(ks[14], (L, DN_CONV, 3 * DN_WIDTH), DN_CONV ** -0.5),
        'dn_a_log': jnp.log(jax.random.uniform(ks[15], (L, 2, DN_HEADS), jnp.float32, 1.0, 16.0)),
        'dn_dt_bias': dt + jnp.log(-jnp.expm1(-dt)),
        'dn_out_gain': 1.0 + nrm(ks[17], (L, DN_HEAD_DIM), 0.05),
        'w_out': nrm(ks[18], (L, MIX_WIDTH, D), MIX_WIDTH ** -0.5),
        'norm_ffn2': 1.0 + nrm(ks[19], (L, D), 0.05),
        'w_ffn2_in': nrm(ks[20], (L, D, 2 * F), D ** -0.5),
        'w_ffn2_out': nrm(ks[21], (L, F, D), F ** -0.5),
    }


def reference(x, c, ctx, c_ctx, w_mod, b_mod, norm_ffn1, w_ffn1_in, w_ffn1_out, norm_mix, w_in,
              na_q_gain, na_k_gain, na_rpb, dn_conv, dn_a_log, dn_dt_bias, dn_out_gain, w_out,
              norm_ffn2, w_ffn2_in, w_ffn2_out):
    n_tok = x.shape[1]
    rope_cos, rope_sin = axial_rope_tables(n_tok, DN_HEAD_DIM)
    silu_c = jax.nn.silu(c)
    silu_cc = jax.nn.silu(c_ctx)
    for l in range(DEPTH):
        ctx_out = l < DEPTH - 1
        mod_lat = jnp.split((silu_c @ w_mod[l] + b_mod[l])[:, None, :], N_MOD, axis=-1)
        mod_ctx = jnp.split((silu_cc @ w_mod[l] + b_mod[l])[None, None, :], N_MOD, axis=-1)
        x = macaron_half(x, mod_lat[0:3], norm_ffn1[l], w_ffn1_in[l], w_ffn1_out[l])
        ctx = macaron_half(ctx, mod_ctx[0:3], norm_ffn1[l], w_ffn1_in[l], w_ffn1_out[l])
        h_lat = modulate(rms_norm(x, norm_mix[l]), mod_lat[3], mod_lat[4])
        h_ctx = modulate(rms_norm(ctx, norm_mix[l]), mod_ctx[3], mod_ctx[4])
        y_lat, y_ctx = hybrid_mixer(h_lat, h_ctx, w_in[l], na_q_gain[l], na_k_gain[l], na_rpb[l],
                                    dn_conv[l], dn_a_log[l], dn_dt_bias[l], dn_out_gain[l],
                                    rope_cos, rope_sin, ctx_out)
        x = x + mod_lat[5] * (y_lat @ w_out[l])
        x = macaron_half(x, mod_lat[6:9], norm_ffn2[l], w_ffn2_in[l], w_ffn2_out[l])
        if ctx_out:
            ctx = ctx + mod_ctx[5] * (y_ctx @ w_out[l])
            ctx = macaron_half(ctx, mod_ctx[6:9], norm_ffn2[l], w_ffn2_in[l], w_ffn2_out[l])
    return x
```

```python
import functools

import jax
import jax.numpy as jnp
from jax import lax
from jax.experimental import pallas as pl
from jax.experimental.pallas import tpu as pltpu

F32 = jnp.float32
BF16 = jnp.bfloat16

GRID_W = 64
NA_HEADS = 8
NA_HEAD_DIM = 64
NA_WIDTH = NA_HEADS * NA_HEAD_DIM
NA_WIN_ROWS = 8
NA_WIN_COLS = 16
DN_HEADS = 4
DN_HEAD_DIM = 128
DN_WIDTH = DN_HEADS * DN_HEAD_DIM
DN_CONV = 5
DN_CHUNK = 64
N_MOD = 9
ROPE_BASE = 10000.0
RMS_EPS = 1e-6
LANES = 128
NEG_BIG = -1e30
VMEM_LIMIT = 56 * 1024 * 1024


def _cparams(sem):
    return pltpu.CompilerParams(dimension_semantics=sem, vmem_limit_bytes=VMEM_LIMIT)


def _dot(a, b):
    return jnp.dot(a, b, preferred_element_type=F32)


def _dot_nt(a, b):
    return lax.dot_general(a, b, (((1,), (1,)), ((), ())), preferred_element_type=F32)


def _dot_tn(a, b):
    return lax.dot_general(a, b, (((0,), (0,)), ((), ())), preferred_element_type=F32)


def _silu(x):
    return x * jax.nn.sigmoid(x)


def _norm_mod(x, gain, shift, scale):
    ms = jnp.mean(x * x, axis=-1, keepdims=True)
    y = x * lax.rsqrt(ms + RMS_EPS) * gain
    return y * (1.0 + scale) + shift


def _mod_kernel(c_ref, w_ref, b_ref, o_ref):
    s = _silu(c_ref[...]).astype(BF16)
    o_ref[0] = _dot(s, w_ref[0].astype(BF16)) + b_ref[0]


def _mod_all(c_rows, w_mod, b_mod):
    n_layers, d, n = w_mod.shape
    r = c_rows.shape[0]
    tn = n // 8 if (n // 8) % LANES == 0 else n
    return pl.pallas_call(
        _mod_kernel,
        grid=(n_layers, n // tn),
        in_specs=[pl.BlockSpec((r, d), lambda l, j: (0, 0)),
                  pl.BlockSpec((1, d, tn), lambda l, j: (l, 0, j)),
                  pl.BlockSpec((1, 1, tn), lambda l, j: (l, 0, j))],
        out_specs=pl.BlockSpec((1, r, tn), lambda l, j: (l, 0, j)),
        out_shape=jax.ShapeDtypeStruct((n_layers, r, n), F32),
        compiler_params=_cparams(("parallel", "parallel")),
        name="mod",
    )(c_rows, w_mod, b_mod.reshape(n_layers, 1, n))


def _ffn_kernel(x_ref, mod_ref, gain_ref, wg_ref, wu_ref, wo_ref, o_ref, h_scr, acc_scr):
    j = pl.program_id(2)

    @pl.when(j == 0)
    def _():
        h = _norm_mod(x_ref[0], gain_ref[...], mod_ref[0, 0:1, :], mod_ref[0, 1:2, :])
        h_scr[...] = h.astype(BF16)
        acc_scr[...] = jnp.zeros_like(acc_scr)

    h = h_scr[...]
    g = _dot(h, wg_ref[0])
    u = _dot(h, wu_ref[0])
    a = (_silu(g) * u).astype(BF16)
    acc_scr[...] += _dot(a, wo_ref[0])

    @pl.when(j == pl.num_programs(2) - 1)
    def _():
        o_ref[0] = x_ref[0] + (0.5 * mod_ref[0, 2:3, :]) * acc_scr[...]


def _ffn_tiles(t, f):
    tm = min(t, 512)
    tf = 256 if f % 256 == 0 else LANES
    return tm, tf


def _ffn(x, mod3, gain, w_in, w_out, layer):
    b, t, d = x.shape
    f = w_out.shape[1]
    tm, tf = _ffn_tiles(t, f)
    nf = f // tf
    return pl.pallas_call(
        _ffn_kernel,
        grid=(b, t // tm, nf),
        in_specs=[pl.BlockSpec((1, tm, d), lambda bi, i, j: (bi, i, 0)),
                  pl.BlockSpec((1, 3, d), lambda bi, i, j: (bi, 0, 0)),
                  pl.BlockSpec((1, d), lambda bi, i, j: (0, 0)),
                  pl.BlockSpec((1, d, tf), lambda bi, i, j: (layer, 0, j)),
                  pl.BlockSpec((1, d, tf), lambda bi, i, j: (layer, 0, nf + j)),
                  pl.BlockSpec((1, tf, d), lambda bi, i, j: (layer, j, 0))],
        out_specs=pl.BlockSpec((1, tm, d), lambda bi, i, j: (bi, i, 0)),
        out_shape=jax.ShapeDtypeStruct(x.shape, F32),
        scratch_shapes=[pltpu.VMEM((tm, d), BF16), pltpu.VMEM((tm, d), F32)],
        compiler_params=_cparams(("parallel", "parallel", "arbitrary")),
        name="ffn",
    )(x, mod3, gain.reshape(1, d), w_in, w_in, w_out)


def _mix_in_kernel(x_ref, mod_ref, gain_ref, w_ref, seg_ref, qg_ref, kg_ref, alog_ref, dtb_ref,
                   q_ref, k_ref, v_ref, dqkv_ref, dgate_ref, bg_ref):
    h = _norm_mod(x_ref[0], gain_ref[...], mod_ref[0, 0:1, :], mod_ref[0, 1:2, :]).astype(BF16)
    na = NA_WIDTH

    def head_rms(p, gain):
        ss = _dot((p * p).astype(BF16), seg_ref[...])
        return p * lax.rsqrt(ss * (1.0 / NA_HEAD_DIM) + RMS_EPS) * gain

    q_ref[0] = head_rms(_dot(h, w_ref[0, :, 0:na]), qg_ref[...]).astype(BF16)
    k_ref[0] = head_rms(_dot(h, w_ref[0, :, na:2 * na]), kg_ref[...]).astype(BF16)
    v_ref[0] = _dot(h, w_ref[0, :, 2 * na:3 * na]).astype(BF16)
    off = 3 * na
    dqkv_ref[0] = _dot(h, w_ref[0, :, off:off + 3 * DN_WIDTH])
    off += 3 * DN_WIDTH
    dgate_ref[0] = _dot(h, w_ref[0, :, off:off + DN_WIDTH])
    off += DN_WIDTH
    pb = _dot(h, w_ref[0, :, off:off + LANES])
    lane = lax.broadcasted_iota(jnp.int32, pb.shape, 1)
    z = pb + dtb_ref[...]
    softplus = jnp.maximum(z, 0.0) + jnp.log1p(jnp.exp(-jnp.abs(z)))
    bg_ref[0] = jnp.where(lane < 2 * DN_HEADS, jax.nn.sigmoid(pb), -jnp.exp(alog_ref[...]) * softplus)


def _mix_in(x, mod2, gain, w_pad, seg, qg, kg, alog, dtb, layer):
    b, t, d = x.shape
    tm = min(t, 512)
    ncol = w_pad.shape[2]
    row = lambda bi, i: (bi, i, 0)
    const2 = lambda bi, i: (0, 0)
    outs = [jax.ShapeDtypeStruct((b, t, NA_WIDTH), BF16)] * 3 + [
        jax.ShapeDtypeStruct((b, t, 3 * DN_WIDTH), F32),
        jax.ShapeDtypeStruct((b, t, DN_WIDTH), F32),
        jax.ShapeDtypeStruct((b, t, LANES), F32)]
    return pl.pallas_call(
        _mix_in_kernel,
        grid=(b, t // tm),
        in_specs=[pl.BlockSpec((1, tm, d), row),
                  pl.BlockSpec((1, 2, d), lambda bi, i: (bi, 0, 0)),
                  pl.BlockSpec((1, d), const2),
                  pl.BlockSpec((1, d, ncol), lambda bi, i: (layer, 0, 0)),
                  pl.BlockSpec((NA_WIDTH, NA_WIDTH), const2),
                  pl.BlockSpec((1, NA_WIDTH), const2),
                  pl.BlockSpec((1, NA_WIDTH), const2),
                  pl.BlockSpec((1, LANES), const2),
                  pl.BlockSpec((1, LANES), const2)],
        out_specs=[pl.BlockSpec((1, tm, NA_WIDTH), row)] * 3 + [
            pl.BlockSpec((1, tm, 3 * DN_WIDTH), row),
            pl.BlockSpec((1, tm, DN_WIDTH), row),
            pl.BlockSpec((1, tm, LANES), row)],
        out_shape=outs,
        compiler_params=_cparams(("parallel", "parallel")),
        name="mix_in",
    )(x, mod2, gain.reshape(1, d), w_pad, seg, qg, kg, alog, dtb)


def _softmax_pv(scores, values):
    m = functools.reduce(jnp.maximum, [s.max(axis=-1, keepdims=True) for s in scores])
    ps = [jnp.exp(s - m) for s in scores]
    denom = functools.reduce(jnp.add, [p.sum(axis=-1, keepdims=True) for p in ps])
    o = functools.reduce(jnp.add, [_dot(p.astype(BF16), v) for p, v in zip(ps, values)])
    return o / denom


def _na_kernel(q_ref, k_ref, v_ref, kc_ref, vc_ref, bias_ref, o_ref, *, rows):
    r = pl.program_id(1)
    start = jnp.clip(r - NA_WIN_ROWS // 2, 0, rows - NA_WIN_ROWS)
    base = pl.multiple_of(start * GRID_W, GRID_W)
    n_win = NA_WIN_ROWS * GRID_W
    lo = lax.broadcasted_iota(jnp.int32, (GRID_W, LANES), 1) < NA_HEAD_DIM
    for pair in range(NA_HEADS // 2):
        cols = slice(LANES * pair, LANES * (pair + 1))
        q2 = q_ref[0, :, cols]
        k2 = k_ref[0, pl.ds(base, n_win), cols]
        v2 = v_ref[0, pl.ds(base, n_win), cols]
        kc2 = kc_ref[0, :, cols]
        vc2 = vc_ref[0, :, cols]
        halves = []
        for e in range(2):
            qm = jnp.where(lo if e == 0 else jnp.logical_not(lo), q2, jnp.zeros_like(q2))
            s_lat = _dot_nt(qm, k2) + bias_ref[0, 2 * pair + e]
            s_ctx = _dot_nt(qm, kc2)
            halves.append(_softmax_pv([s_lat, s_ctx], [v2, vc2]))
        o_ref[0, :, cols] = jnp.where(lo, halves[0], halves[1]).astype(BF16)


def _na_bias_tables(rpb):
    cols = jnp.arange(GRID_W)
    col_start = jnp.clip(cols - NA_WIN_COLS // 2, 0, GRID_W - NA_WIN_COLS)
    col_ok = (cols[None, :] >= col_start[:, None]) & (cols[None, :] < col_start[:, None] + NA_WIN_COLS)
    dc_idx = jnp.clip(cols[None, :] - cols[:, None] + NA_WIN_COLS - 1, 0, 2 * NA_WIN_COLS - 2)
    rpb_cols = rpb[:, :, dc_idx]
    rpb_cols = jnp.where(col_ok[None, None], rpb_cols, NEG_BIG)
    tabs = []
    for d0 in range(NA_WIN_ROWS):
        blk = rpb_cols[:, d0:d0 + NA_WIN_ROWS]
        tabs.append(jnp.transpose(blk, (0, 2, 1, 3)).reshape(NA_HEADS, GRID_W, NA_WIN_ROWS * GRID_W))
    return jnp.stack(tabs).astype(F32)


def _na_lat(q, k, v, kc, vc, bias):
    b, t, w = q.shape
    rows = t // GRID_W
    c = kc.shape[1]
    half = NA_WIN_ROWS // 2

    def bias_idx(bi, r):
        start = jnp.clip(r - half, 0, rows - NA_WIN_ROWS)
        return (start - r + NA_WIN_ROWS - 1, 0, 0, 0)

    full = lambda bi, r: (bi, 0, 0)
    return pl.pallas_call(
        functools.partial(_na_kernel, rows=rows),
        grid=(b, rows),
        in_specs=[pl.BlockSpec((1, GRID_W, w), lambda bi, r: (bi, r, 0)),
                  pl.BlockSpec((1, t, w), full),
                  pl.BlockSpec((1, t, w), full),
                  pl.BlockSpec((1, c, w), full),
                  pl.BlockSpec((1, c, w), full),
                  pl.BlockSpec((1, NA_HEADS, GRID_W, NA_WIN_ROWS * GRID_W), bias_idx)],
        out_specs=pl.BlockSpec((1, GRID_W, w), lambda bi, r: (bi, r, 0)),
        out_shape=jax.ShapeDtypeStruct((b, t, w), BF16),
        compiler_params=_cparams(("parallel", "arbitrary")),
        name="na_lat",
    )(q, k, v, kc, vc, bias)


def _na_ctx_kernel(q_ref, k_ref, v_ref, o_ref):
    m = q_ref.shape[1]
    lo = lax.broadcasted_iota(jnp.int32, (m, LANES), 1) < NA_HEAD_DIM
    for pair in range(NA_HEADS // 2):
        cols = slice(LANES * pair, LANES * (pair + 1))
        q2 = q_ref[0, :, cols]
        k2 = k_ref[0, :, cols]
        v2 = v_ref[0, :, cols]
        halves = []
        for e in range(2):
            qm = jnp.where(lo if e == 0 else jnp.logical_not(lo), q2, jnp.zeros_like(q2))
            halves.append(_softmax_pv([_dot_nt(qm, k2)], [v2]))
        o_ref[0, :, cols] = jnp.where(lo, halves[0], halves[1]).astype(BF16)


def _na_ctx(q, k, v):
    b, c, w = q.shape
    spec = pl.BlockSpec((1, c, w), lambda bi: (bi, 0, 0))
    return pl.pallas_call(
        _na_ctx_kernel, grid=(b,), in_specs=[spec] * 3, out_specs=spec,
        out_shape=jax.ShapeDtypeStruct((b, c, w), BF16),
        compiler_params=_cparams(("parallel",)), name="na_ctx",
    )(q, k, v)


def _dn_prep_kernel(x_ref, w_ref, cos_ref, sin_ref, o_ref, *, rope):
    j = pl.program_id(1)
    x = x_ref[0]
    t = x.shape[0]
    pad = jnp.zeros((8, LANES), F32)
    xp = jnp.concatenate([pad, x, pad], axis=0)
    half = DN_CONV // 2
    y = jnp.zeros_like(x)
    for tap in range(DN_CONV):
        lo = 8 + tap - half
        y = y + w_ref[tap:tap + 1, :] * xp[lo:lo + t, :]
    y = _silu(y)
    n = y * lax.rsqrt(jnp.sum(y * y, axis=-1, keepdims=True) + RMS_EPS)
    n = n * jnp.where(j < DN_HEADS, DN_HEAD_DIM ** -0.5, 1.0)
    if rope:
        lane = lax.broadcasted_iota(jnp.int32, n.shape, 1)
        first = (lane & (DN_HEAD_DIM // 4)) == 0
        partner = jnp.where(first, pltpu.roll(n, LANES - DN_HEAD_DIM // 4, 1),
                            pltpu.roll(n, DN_HEAD_DIM // 4, 1))
        n = n * cos_ref[...] + partner * sin_ref[...]
    o_ref[0] = jnp.where(j < 2 * DN_HEADS, n, y).astype(BF16)


def _dn_prep(dqkv, conv_w, cos, sin_signed, rope):
    b, t, w = dqkv.shape
    blk = pl.BlockSpec((1, t, LANES), lambda bi, j: (bi, 0, j))
    tab = pl.BlockSpec((t, LANES), lambda bi, j: (0, 0))
    return pl.pallas_call(
        functools.partial(_dn_prep_kernel, rope=rope),
        grid=(b, w // LANES),
        in_specs=[blk, pl.BlockSpec((DN_CONV, LANES), lambda bi, j: (0, j)), tab, tab],
        out_specs=blk,
        out_shape=jax.ShapeDtypeStruct((b, t, w), BF16),
        compiler_params=_cparams(("parallel", "parallel")),
        name="dn_prep",
    )(dqkv, conv_w, cos, sin_signed)


def _rope_tables(n_tok):
    n_freq = DN_HEAD_DIM // 4
    inv_freq = ROPE_BASE ** (-jnp.arange(n_freq, dtype=F32) / n_freq)
    tok = jnp.arange(n_tok)
    row = (tok // GRID_W).astype(F32)[:, None] * inv_freq
    col = (tok % GRID_W).astype(F32)[:, None] * inv_freq
    ang = jnp.concatenate([row, row, col, col], axis=-1)
    sign = jnp.where((jnp.arange(DN_HEAD_DIM) & n_freq) == 0, -1.0, 1.0).astype(F32)
    return jnp.cos(ang), jnp.sin(ang) * sign


def _chunk_decay(g_col, g_row, reverse):
    c = DN_CHUNK
    ii = lax.broadcasted_iota(jnp.int32, (c, c), 0)
    jj = lax.broadcasted_iota(jnp.int32, (c, c), 1)
    incl = (ii <= jj) if reverse else (ii >= jj)
    gc_col = jnp.sum(jnp.where(incl, g_row, 0.0), axis=1, keepdims=True)
    incl_t = (ii >= jj) if reverse else (ii <= jj)
    gc_row = jnp.sum(jnp.where(incl_t, g_col, 0.0), axis=0, keepdims=True)
    gamma = jnp.exp(jnp.where(incl, gc_col - gc_row, NEG_BIG))
    strict = (ii < jj) if reverse else (ii > jj)
    return gc_col, gc_row, gamma, incl, strict


def _dn_gram_kernel(k_ref, sc_ref, sr_ref, a_ref, *, n_sub):
    for s in range(n_sub):
        rows = slice(s * DN_CHUNK, (s + 1) * DN_CHUNK)
        k = k_ref[0, rows, :]
        kk = _dot_nt(k, k)
        acc = jnp.zeros((DN_CHUNK, DN_CHUNK), F32)
        for direction in range(2):
            beta_col = sc_ref[0, 0, rows, direction:direction + 1]
            g_col = sc_ref[0, 0, rows, 2 + direction:3 + direction]
            g_row = sr_ref[0, 0, s, 2 + direction:3 + direction, :]
            _, _, gamma, _, strict = _chunk_decay(g_col, g_row, direction == 1)
            acc = acc + jnp.where(strict, beta_col * kk * gamma, 0.0)
        a_ref[0, 0, s] = acc


def _dn_gram(qkv, sc, sr):
    b, t, _ = qkv.shape
    n = t // DN_CHUNK
    n_sub = 2
    return pl.pallas_call(
        functools.partial(_dn_gram_kernel, n_sub=n_sub),
        grid=(b, DN_HEADS, n // n_sub),
        in_specs=[pl.BlockSpec((1, n_sub * DN_CHUNK, LANES), lambda bi, h, i: (bi, i, DN_HEADS + h)),
                  pl.BlockSpec((1, 1, n_sub * DN_CHUNK, 4), lambda bi, h, i: (bi, h, i, 0)),
                  pl.BlockSpec((1, 1, n_sub, 4, DN_CHUNK), lambda bi, h, i: (bi, h, i, 0, 0))],
        out_specs=pl.BlockSpec((1, 1, n_sub, DN_CHUNK, DN_CHUNK), lambda bi, h, i: (bi, h, i, 0, 0)),
        out_shape=jax.ShapeDtypeStruct((b, DN_HEADS, n, DN_CHUNK, DN_CHUNK), F32),
        compiler_params=_cparams(("parallel", "parallel", "parallel")),
        name="dn_gram",
    )(qkv, sc, sr)


def _dn_inv_kernel(a_ref, o_ref, tf_scr, tb_scr):
    c = DN_CHUNK
    sub = lax.broadcasted_iota(jnp.int32, (c, LANES), 0)
    tf_scr[...] = jnp.zeros_like(tf_scr)
    tb_scr[...] = jnp.zeros_like(tb_scr)
    tf_scr[0] = (sub == 0).astype(F32)
    tb_scr[c - 1] = (sub == c - 1).astype(F32)
    for i in range(1, c):
        hi = (i // 8 + 1) * 8

        def body_f(j, acc, i=i, hi=hi):
            return acc - a_ref[i, pl.ds(j, 1), :] * tf_scr[j, 0:hi, :]

        tf_scr[i, 0:hi, :] = lax.fori_loop(0, i, body_f, (sub[0:hi] == i).astype(F32))
    for i in range(c - 2, -1, -1):
        lo = (i // 8) * 8

        def body_b(j, acc, i=i, lo=lo):
            return acc - a_ref[i, pl.ds(j, 1), :] * tb_scr[j, lo:c, :]

        tb_scr[i, lo:c, :] = lax.fori_loop(i + 1, c, body_b, (sub[lo:c] == i).astype(F32))
    for i in range(c):
        o_ref[i] = jnp.where(sub <= i, tf_scr[i], tb_scr[i])


def _dn_inv(a_t):
    c, _, ns = a_t.shape
    spec = pl.BlockSpec((c, c, LANES), lambda i: (0, 0, i))
    return pl.pallas_call(
        _dn_inv_kernel, grid=(ns // LANES,), in_specs=[spec], out_specs=spec,
        out_shape=jax.ShapeDtypeStruct(a_t.shape, F32),
        scratch_shapes=[pltpu.VMEM((c, c, LANES), F32), pltpu.VMEM((c, c, LANES), F32)],
        compiler_params=_cparams(("parallel",)), name="dn_inv",
    )(a_t)


def _dn_solve(a_pack):
    shp = a_pack.shape
    ns = shp[0] * shp[1] * shp[2]
    ns_pad = -(-ns // LANES) * LANES
    a_t = jnp.transpose(a_pack.reshape(ns, DN_CHUNK, DN_CHUNK), (1, 2, 0))
    if ns_pad != ns:
        a_t = jnp.pad(a_t, ((0, 0), (0, 0), (0, ns_pad - ns)))
    t_t = _dn_inv(a_t)[:, :, :ns]
    return jnp.transpose(t_t, (2, 0, 1)).reshape(shp)


def _dn_chunk_step(q, k, v, t_pack, sc, sr, state, reverse):
    d = 1 if reverse else 0
    beta_row = sr[d:d + 1, :]
    g_col = sc[:, 2 + d:3 + d]
    g_row = sr[2 + d:3 + d, :]
    gc_col, gc_row, gamma, incl, strict = _chunk_decay(g_col, g_row, reverse)
    ii = lax.broadcasted_iota(jnp.int32, (DN_CHUNK, DN_CHUNK), 0)
    jj = lax.broadcasted_iota(jnp.int32, (DN_CHUNK, DN_CHUNK), 1)
    t_inv = jnp.where(strict, t_pack, jnp.where(ii == jj, 1.0, 0.0))
    u = _dot((t_inv * beta_row).astype(BF16), v)
    w = _dot((t_inv * (beta_row * jnp.exp(gc_row))).astype(BF16), k)
    qk = jnp.where(incl, _dot_nt(q, k) * gamma, 0.0)
    s16 = state.astype(BF16)
    v_new = u - _dot(w.astype(BF16), s16)
    o = jnp.exp(gc_col) * _dot(q, s16) + _dot(qk.astype(BF16), v_new.astype(BF16))
    g_tot = gc_col[0:1, :] if reverse else gc_col[DN_CHUNK - 1:DN_CHUNK, :]
    kv = _dot_tn(k, (jnp.exp(g_tot - gc_col) * v_new).astype(BF16))
    return o, state * jnp.exp(g_tot) + kv


def _dn_scan_kernel(ql_ref, kl_ref, vl_ref, tl_ref, scl_ref, srl_ref, gl_ref,
                    qc_ref, kc_ref, vc_ref, tc_ref, scc_ref, src_ref, gc_ref, gain_ref,
                    yl_ref, yc_ref, ol_scr, oc_scr, sf_scr, sb_scr):
    sf_scr[...] = jnp.zeros_like(sf_scr)
    sb_scr[...] = jnp.zeros_like(sb_scr)
    ol_scr[...] = jnp.zeros_like(ol_scr)
    oc_scr[...] = jnp.zeros_like(oc_scr)

    def run(q_ref, k_ref, v_ref, t_ref, sc_ref, sr_ref, o_scr):
        n = o_scr.shape[0] // DN_CHUNK

        def body(it, carry):
            for reverse, s_scr in ((False, sf_scr), (True, sb_scr)):
                m = (n - 1 - it) if reverse else it
                rows = pl.ds(pl.multiple_of(m * DN_CHUNK, DN_CHUNK), DN_CHUNK)
                o, s_new = _dn_chunk_step(
                    q_ref[0, rows, :], k_ref[0, rows, :], v_ref[0, rows, :], t_ref[0, 0, m],
                    sc_ref[0, 0, rows, :], sr_ref[0, 0, m], s_scr[...], reverse)
                s_scr[...] = s_new
                o_scr[rows, :] += o
            return carry

        lax.fori_loop(0, n, body, 0)

    run(qc_ref, kc_ref, vc_ref, tc_ref, scc_ref, src_ref, oc_scr)
    run(ql_ref, kl_ref, vl_ref, tl_ref, scl_ref, srl_ref, ol_scr)

    def finish(o_scr, g_ref, y_ref):
        o = o_scr[...]
        y = o * lax.rsqrt(jnp.mean(o * o, axis=-1, keepdims=True) + RMS_EPS) * gain_ref[...]
        y_ref[0] = (y * _silu(g_ref[0])).astype(BF16)

    finish(ol_scr, gl_ref, yl_ref)
    finish(oc_scr, gc_ref, yc_ref)


def _dn_scan(qkv_l, t_l, sc_l, sr_l, gate_l, qkv_c, t_c, sc_c, sr_c, gate_c, out_gain):
    b, t, _ = qkv_l.shape
    c = qkv_c.shape[1]
    nh = DN_HEADS

    def specs(tok, n):
        col = lambda off: pl.BlockSpec((1, tok, LANES), lambda bi, h, off=off: (bi, 0, off + h))
        return [col(0), col(nh), col(2 * nh),
                pl.BlockSpec((1, 1, n, DN_CHUNK, DN_CHUNK), lambda bi, h: (bi, h, 0, 0, 0)),
                pl.BlockSpec((1, 1, tok, 4), lambda bi, h: (bi, h, 0, 0)),
                pl.BlockSpec((1, 1, n, 4, DN_CHUNK), lambda bi, h: (bi, h, 0, 0, 0)),
                col(0)]

    out_l = pl.BlockSpec((1, t, LANES), lambda bi, h: (bi, 0, h))
    out_c = pl.BlockSpec((1, c, LANES), lambda bi, h: (bi, 0, h))
    return pl.pallas_call(
        _dn_scan_kernel,
        grid=(b, nh),
        in_specs=specs(t, t // DN_CHUNK) + specs(c, c // DN_CHUNK)
        + [pl.BlockSpec((1, LANES), lambda bi, h: (0, 0))],
        out_specs=[out_l, out_c],
        out_shape=[jax.ShapeDtypeStruct((b, t, DN_WIDTH), BF16),
                   jax.ShapeDtypeStruct((b, c, DN_WIDTH), BF16)],
        scratch_shapes=[pltpu.VMEM((t, LANES), F32), pltpu.VMEM((c, LANES), F32),
                        pltpu.VMEM((DN_HEAD_DIM, DN_HEAD_DIM), F32),
                        pltpu.VMEM((DN_HEAD_DIM, DN_HEAD_DIM), F32)],
        compiler_params=_cparams(("parallel", "parallel")),
        name="dn_scan",
    )(qkv_l, qkv_l, qkv_l, t_l, sc_l, sr_l, gate_l,
      qkv_c, qkv_c, qkv_c, t_c, sc_c, sr_c, gate_c, out_gain.reshape(1, LANES))


def _head_scalars(bg):
    b, t, _ = bg.shape
    s = bg[:, :, :4 * DN_HEADS].reshape(b, t, 4, DN_HEADS)
    rows = s.reshape(b, t // DN_CHUNK, DN_CHUNK, 4, DN_HEADS)
    return jnp.transpose(s, (0, 3, 1, 2)), jnp.transpose(rows, (0, 4, 1, 3, 2))


def _out_kernel(x_ref, mod_ref, na_ref, dn_ref, w_ref, o_ref):
    y = _dot(na_ref[0], w_ref[0, 0:NA_WIDTH, :]) + _dot(dn_ref[0], w_ref[0, NA_WIDTH:, :])
    o_ref[0] = x_ref[0] + mod_ref[0] * y


def _out_proj(x, gate, na, dn, w_out, layer):
    b, t, d = x.shape
    tm = min(t, 512)
    row = lambda bi, i: (bi, i, 0)
    return pl.pallas_call(
        _out_kernel,
        grid=(b, t // tm),
        in_specs=[pl.BlockSpec((1, tm, d), row),
                  pl.BlockSpec((1, 1, d), lambda bi, i: (bi, 0, 0)),
                  pl.BlockSpec((1, tm, NA_WIDTH), row),
                  pl.BlockSpec((1, tm, DN_WIDTH), row),
                  pl.BlockSpec((1, NA_WIDTH + DN_WIDTH, d), lambda bi, i: (layer, 0, 0))],
        out_specs=pl.BlockSpec((1, tm, d), row),
        out_shape=jax.ShapeDtypeStruct(x.shape, F32),
        compiler_params=_cparams(("parallel", "parallel")),
        name="out_proj",
    )(x, gate, na, dn, w_out)


def kernel(x, c, ctx, c_ctx, w_mod, b_mod, norm_ffn1, w_ffn1_in, w_ffn1_out, norm_mix, w_in,
           na_q_gain, na_k_gain, na_rpb, dn_conv, dn_a_log, dn_dt_bias, dn_out_gain, w_out,
           norm_ffn2, w_ffn2_in, w_ffn2_out):
    n_layers = w_mod.shape[0]
    b, t, d = x.shape
    n_ctx = ctx.shape[1]
    assert t % GRID_W == 0 and t // GRID_W >= NA_WIN_ROWS
    assert t % (2 * DN_CHUNK) == 0 and n_ctx % (2 * DN_CHUNK) == 0
    assert d == NA_WIDTH + DN_WIDTH

    w1i, w1o = w_ffn1_in.astype(BF16), w_ffn1_out.astype(BF16)
    w2i, w2o = w_ffn2_in.astype(BF16), w_ffn2_out.astype(BF16)
    in_cols = w_in.shape[2]
    pad_cols = 3 * NA_WIDTH + 4 * DN_WIDTH + LANES - in_cols
    w_in_pad = jnp.pad(w_in, ((0, 0), (0, 0), (0, pad_cols))).astype(BF16)
    w_out16 = w_out.astype(BF16)
    seg = (jnp.arange(NA_WIDTH)[:, None] // NA_HEAD_DIM == jnp.arange(NA_WIDTH)[None, :] // NA_HEAD_DIM)
    seg = seg.astype(BF16)
    cos, sin_signed = _rope_tables(t)
    no_rope = jnp.zeros((n_ctx, LANES), F32)

    n_rows = -(-(b + 1) // 8) * 8
    c_rows = jnp.zeros((n_rows, d), F32).at[:b].set(c).at[b].set(c_ctx)
    mods = _mod_all(c_rows, w_mod, b_mod)

    for l in range(n_layers):
        mod_lat = mods[l, :b].reshape(b, N_MOD, d)
        mod_ctx = jnp.broadcast_to(mods[l, b].reshape(1, N_MOD, d), (b, N_MOD, d))
        ctx_out = l < n_layers - 1

        x = _ffn(x, mod_lat[:, 0:3], norm_ffn1[l], w1i, w1o, l)
        ctx = _ffn(ctx, mod_ctx[:, 0:3], norm_ffn1[l], w1i, w1o, l)

        qg = (jnp.tile(na_q_gain[l], NA_HEADS) * NA_HEAD_DIM ** -0.5).reshape(1, NA_WIDTH)
        kg = jnp.tile(na_k_gain[l], NA_HEADS).reshape(1, NA_WIDTH)
        lane_pad = (0, LANES - 4 * DN_HEADS)
        alog = jnp.pad(jnp.concatenate([jnp.zeros(2 * DN_HEADS, F32), dn_a_log[l].reshape(-1)]), lane_pad)
        dtb = jnp.pad(jnp.concatenate([jnp.zeros(2 * DN_HEADS, F32), dn_dt_bias[l].reshape(-1)]), lane_pad)
        alog, dtb = alog.reshape(1, LANES), dtb.reshape(1, LANES)
        mix = functools.partial(_mix_in, gain=norm_mix[l], w_pad=w_in_pad, seg=seg, qg=qg, kg=kg,
                                alog=alog, dtb=dtb, layer=l)
        ql, kl, vl, dqkv_l, dgate_l, bg_l = mix(x, mod_lat[:, 3:5])
        qc, kc, vc, dqkv_c, dgate_c, bg_c = mix(ctx, mod_ctx[:, 3:5])

        na_l = _na_lat(ql, kl, vl, kc, vc, _na_bias_tables(na_rpb[l]))

        qkv_l = _dn_prep(dqkv_l, dn_conv[l], cos, sin_signed, rope=True)
        qkv_c = _dn_prep(dqkv_c, dn_conv[l], no_rope, no_rope, rope=False)
        sc_l, sr_l = _head_scalars(bg_l)
        sc_c, sr_c = _head_scalars(bg_c)
        t_l = _dn_solve(_dn_gram(qkv_l, sc_l, sr_l))
        t_c = _dn_solve(_dn_gram(qkv_c, sc_c, sr_c))
        dn_l, dn_c = _dn_scan(qkv_l, t_l, sc_l, sr_l, dgate_l, qkv_c, t_c, sc_c, sr_c, dgate_c,
                              dn_out_gain[l])

        x = _out_proj(x, mod_lat[:, 5:6], na_l, dn_l, w_out16, l)
        x = _ffn(x, mod_lat[:, 6:9], norm_ffn2[l], w2i, w2o, l)
        if ctx_out:
            na_c = _na_ctx(qc, kc, vc)
            ctx = _out_proj(ctx, mod_ctx[:, 5:6], na_c, dn_c, w_out16, l)
            ctx = _ffn(ctx, mod_ctx[:, 6:9], norm_ffn2[l], w2i, w2o, l)
    return x
```
